```python
import jax, jax.numpy as jnp
from jax import lax
import numpy as np

D_MODEL = 1024
BATCH = 8
SEQ = 8192
DEPTH = 2

GRID_W = 64
CTX_LEN = 256
N_MIXERS = 2
N_NA_LAYERS = (DEPTH + N_MIXERS - 1) // N_MIXERS
N_MLA_LAYERS = DEPTH // N_MIXERS
NORM_EPS = 1e-6
NEG_INF = -1e30

NA_HEADS = 16
NA_HEAD_DIM = D_MODEL // NA_HEADS
NA_KH = 8
NA_KW = 16
NA_QC = 16
NA_KC = NA_QC + NA_KW
NA_SCALE = NA_HEAD_DIM ** -0.5

MLA_HEADS = 16
MLA_NOPE_DIM = 64
MLA_ROPE_DIM = 32
MLA_V_DIM = 64
MLA_Q_LORA = 256
MLA_KV_LORA = 128
MLA_SCALE = (MLA_NOPE_DIM + MLA_ROPE_DIM) ** -0.5
ROPE_THETA = 10000.0
Q_BLOCK = 128

N_EXPERTS = 32
TOP_K = 4
D_FF = D_MODEL
SWIGLU_ALPHA = 1.702
SWIGLU_LIMIT = 7.0
MOE_BLOCK = 256

kernel_name = 'hybrid_natten_mla_moe_dit'


def rmsnorm(x, g):
    xf = x.astype(jnp.float32)
    y = xf * lax.rsqrt(jnp.mean(xf * xf, axis=-1, keepdims=True) + NORM_EPS)
    return (y * g.astype(jnp.float32)).astype(x.dtype)


def modulate(h, shift, scale):
    return h * (1 + scale) + shift


def dense_attend(q, k, v):
    s = jnp.einsum('bqhd,bkhd->bhqk', q, k).astype(jnp.float32)
    p = jax.nn.softmax(s, axis=-1).astype(v.dtype)
    return jnp.einsum('bhqk,bkhd->bqhd', p, v)


def blocked_attend(q, k, v):
    B, S, H, dq = q.shape
    nb = S // Q_BLOCK
    qb = q.reshape(B, nb, Q_BLOCK, H, dq).transpose(1, 0, 2, 3, 4)
    ob = lax.map(lambda qi: dense_attend(qi, k, v), qb)
    return ob.transpose(1, 0, 2, 3, 4).reshape(B, S, H * v.shape[-1])


def axial_rope_tables(n_tokens, dim, dtype):
    t = jnp.arange(n_tokens, dtype=jnp.int32)
    row = (t // GRID_W).astype(jnp.float32)
    col = (t % GRID_W).astype(jnp.float32)
    axis_dim = dim // 2
    inv_freq = ROPE_THETA ** (-jnp.arange(0, axis_dim, 2, dtype=jnp.float32) / axis_dim)
    ang = jnp.concatenate([row[:, None] * inv_freq, col[:, None] * inv_freq], axis=-1)
    return jnp.cos(ang).astype(dtype), jnp.sin(ang).astype(dtype)


def apply_rope(x, cos, sin):
    xp = x.reshape(x.shape[:-1] + (x.shape[-1] // 2, 2))
    x0, x1 = xp[..., 0], xp[..., 1]
    return jnp.stack([x0 * cos - x1 * sin, x0 * sin + x1 * cos], axis=-1).reshape(x.shape)


def neighbourhood_attend(q, k, v, k_ctx, v_ctx, rpb):
    B, R, W, H, hd = q.shape
    kh = min(NA_KH, R)
    n_cb = W // NA_QC
    q_cols = np.arange(W).reshape(n_cb, NA_QC)
    band_start = np.clip(np.arange(n_cb) * NA_QC - NA_KW // 2, 0, W - NA_KC)
    band_cols = band_start[:, None] + np.arange(NA_KC)[None, :]
    win_start = np.clip(q_cols - NA_KW // 2, 0, W - NA_KW)
    bc = band_cols[:, None, :]
    col_ok = (bc >= win_start[:, :, None]) & (bc < win_start[:, :, None] + NA_KW)
    col_bias_idx = np.clip(bc - q_cols[:, :, None] + NA_KW - 1, 0, 2 * NA_KW - 2)
    n_loc = kh * NA_KC

    def row_fn(r):
        ws = jnp.clip(r - kh // 2, 0, R - kh)
        qb = lax.dynamic_index_in_dim(q, r, axis=1, keepdims=False).reshape(B, n_cb, NA_QC, H, hd)
        kb = lax.dynamic_slice_in_dim(k, ws, kh, axis=1)[:, :, band_cols]
        vb = lax.dynamic_slice_in_dim(v, ws, kh, axis=1)[:, :, band_cols]
        row_idx = ws + jnp.arange(kh) - r + NA_KH - 1
        bias = rpb[:, row_idx[:, None, None, None], col_bias_idx[None]]
        bias = bias.transpose(0, 2, 3, 1, 4).astype(jnp.float32)
        s_loc = jnp.einsum('bnqhd,binjhd->bhnqij', qb, kb).astype(jnp.float32) + bias
        s_loc = jnp.where(col_ok[:, :, None, :], s_loc, NEG_INF).reshape(B, H, n_cb, NA_QC, n_loc)
        s_ctx = jnp.einsum('bnqhd,blhd->bhnql', qb, k_ctx).astype(jnp.float32)
        p = jax.nn.softmax(jnp.concatenate([s_loc, s_ctx], axis=-1), axis=-1).astype(v.dtype)
        p_loc = p[..., :n_loc].reshape(B, H, n_cb, NA_QC, kh, NA_KC)
        o = jnp.einsum('bhnqij,binjhd->bnqhd', p_loc, vb) + jnp.einsum('bhnql,blhd->bnqhd', p[..., n_loc:], v_ctx)
        return o.reshape(B, W, H * hd)

    out = lax.map(row_fn, jnp.arange(R, dtype=jnp.int32))
    return out.transpose(1, 0, 2, 3).reshape(B, R * W, H * hd)


def na_mixer(a_lat, a_ctx, w_qkv, rpb, w_o, need_ctx):
    B, S, _ = a_lat.shape
    L = a_ctx.shape[1]
    inner = NA_HEADS * NA_HEAD_DIM
    grid = (B, S // GRID_W, GRID_W, NA_HEADS, NA_HEAD_DIM)
    q, k, v = jnp.split(a_lat @ w_qkv, 3, axis=-1)
    k_ctx, v_ctx = jnp.split(a_ctx @ w_qkv[:, inner:], 2, axis=-1)
    k_ctx = k_ctx.reshape(B, L, NA_HEADS, NA_HEAD_DIM)
    v_ctx = v_ctx.reshape(B, L, NA_HEADS, NA_HEAD_DIM)
    o_lat = neighbourhood_attend(q.reshape(grid) * NA_SCALE, k.reshape(grid), v.reshape(grid), k_ctx, v_ctx, rpb) @ w_o
    if need_ctx:
        q_ctx = (a_ctx @ w_qkv[:, :inner]).reshape(B, L, NA_HEADS, NA_HEAD_DIM) * NA_SCALE
        o_ctx = dense_attend(q_ctx, k_ctx, v_ctx).reshape(B, L, inner) @ w_o
    else:
        o_ctx = None
    return o_lat, o_ctx


def mla_queries(q_lat, q_norm_g, w_qb):
    B, T, _ = q_lat.shape
    q = (rmsnorm(q_lat, q_norm_g) @ w_qb).reshape(B, T, MLA_HEADS, MLA_NOPE_DIM + MLA_ROPE_DIM)
    return q[..., :MLA_NOPE_DIM], q[..., MLA_NOPE_DIM:]


def mla_keys_values(kv_lat, k_rope, kv_norm_g, w_kvb):
    B, T, _ = kv_lat.shape
    kv = (rmsnorm(kv_lat, kv_norm_g) @ w_kvb).reshape(B, T, MLA_HEADS, MLA_NOPE_DIM + MLA_V_DIM)
    k_nope, v = kv[..., :MLA_NOPE_DIM], kv[..., MLA_NOPE_DIM:]
    k_r = jnp.broadcast_to(k_rope[:, :, None, :], (B, T, MLA_HEADS, MLA_ROPE_DIM))
    return jnp.concatenate([k_nope, k_r], axis=-1), v


def mla_mixer(a_lat, a_ctx, w_a, q_norm_g, w_qb, kv_norm_g, w_kvb, w_o, need_ctx):
    B, S, _ = a_lat.shape
    L = a_ctx.shape[1]
    split_at = [MLA_Q_LORA, MLA_Q_LORA + MLA_KV_LORA]
    cos, sin = axial_rope_tables(S, MLA_ROPE_DIM, a_lat.dtype)
    q_l, kv_l, kr_l = jnp.split(a_lat @ w_a, split_at, axis=-1)
    qn, qr = mla_queries(q_l, q_norm_g, w_qb)
    q_lat = jnp.concatenate([qn, apply_rope(qr, cos[:, None, :], sin[:, None, :])], axis=-1) * MLA_SCALE
    k_lat, v_lat = mla_keys_values(kv_l, apply_rope(kr_l, cos, sin), kv_norm_g, w_kvb)
    q_c, kv_c, kr_c = jnp.split(a_ctx @ w_a, split_at, axis=-1)
    k_ctx, v_ctx = mla_keys_values(kv_c, kr_c, kv_norm_g, w_kvb)
    k_all = jnp.concatenate([k_lat, k_ctx], axis=1)
    v_all = jnp.concatenate([v_lat, v_ctx], axis=1)
    o_lat = blocked_attend(q_lat, k_all, v_all) @ w_o
    if need_ctx:
        qn_c, qr_c = mla_queries(q_c, q_norm_g, w_qb)
        q_ctx = jnp.concatenate([qn_c, qr_c], axis=-1) * MLA_SCALE
        o_ctx = dense_attend(q_ctx, k_ctx, v_ctx).reshape(B, L, MLA_HEADS * MLA_V_DIM) @ w_o
    else:
        o_ctx = None
    return o_lat, o_ctx


def clamped_swiglu(h):
    h_glu, h_lin = h[..., ::2], h[..., 1::2]
    h_glu = jnp.minimum(h_glu, SWIGLU_LIMIT)
    h_lin = jnp.clip(h_lin, -SWIGLU_LIMIT, SWIGLU_LIMIT)
    return h_glu * jax.nn.sigmoid(SWIGLU_ALPHA * h_glu) * (h_lin + 1)


def moe(xt, w_r, b_r, w1, b1, w2, b2):
    T, D = xt.shape
    logits = (xt @ w_r + b_r).astype(jnp.float32)
    top_val, top_idx = lax.top_k(logits, TOP_K)
    gate = jax.nn.softmax(top_val, axis=-1).astype(xt.dtype)
    n_assign = T * TOP_K
    flat_e = top_idx.reshape(-1)
    order = jnp.argsort(flat_e)
    sorted_e = flat_e[order]
    counts = jnp.bincount(flat_e, length=N_EXPERTS)
    starts = jnp.cumsum(counts) - counts
    padded = (counts + MOE_BLOCK - 1) // MOE_BLOCK * MOE_BLOCK
    pad_ends = jnp.cumsum(padded)
    pad_starts = pad_ends - padded
    dest = pad_starts[sorted_e] + (jnp.arange(n_assign, dtype=jnp.int32) - starts[sorted_e])
    n_blocks = (n_assign + N_EXPERTS * (MOE_BLOCK - 1) + MOE_BLOCK - 1) // MOE_BLOCK
    n_slots = n_blocks * MOE_BLOCK
    slot_tok = jnp.full((n_slots,), T, jnp.int32).at[dest].set((order // TOP_K).astype(jnp.int32))
    slot_gate = jnp.zeros((n_slots,), xt.dtype).at[dest].set(gate.reshape(-1)[order])
    block_e = jnp.minimum(jnp.searchsorted(pad_ends, jnp.arange(n_blocks, dtype=jnp.int32) * MOE_BLOCK, side='right'), N_EXPERTS - 1)
    x_pad = jnp.concatenate([xt, jnp.zeros((1, D), xt.dtype)], axis=0)

    def block_fn(args):
        tok, e = args
        h = x_pad[tok] @ w1[e] + b1[e]
        return clamped_swiglu(h) @ w2[e] + b2[e]

    y = lax.map(block_fn, (slot_tok.reshape(n_blocks, MOE_BLOCK), block_e)).reshape(n_slots, D)
    return jax.ops.segment_sum(y * slot_gate[:, None], slot_tok, num_segments=T + 1)[:T]


def setup_inputs(seed: int = 0) -> dict:
    key = jax.random.key(seed)
    ks = jax.random.split(key, 24)
    D = D_MODEL
    na_inner = NA_HEADS * NA_HEAD_DIM

    def nrm(k, shape, scale):
        return jax.random.normal(k, shape, jnp.float32) * scale

    return {
        'x': nrm(ks[0], (BATCH, SEQ, D), 1.0),
        'c': nrm(ks[1], (BATCH, D), 1.0),
        'ctx': nrm(ks[2], (BATCH, CTX_LEN, D), 1.0),
        'c_ctx': nrm(ks[3], (D,), 1.0),
        'ada_w': nrm(ks[4], (DEPTH, D, 6 * D), D ** -0.5),
        'ada_b': nrm(ks[5], (DEPTH, 6 * D), 0.02),
        'norm_mix_g': 1.0 + nrm(ks[6], (DEPTH, D), 0.02),
        'norm_ffn_g': 1.0 + nrm(ks[7], (DEPTH, D), 0.02),
        'na_w_qkv': nrm(ks[8], (N_NA_LAYERS, D, 3 * na_inner), D ** -0.5),
        'na_rpb': nrm(ks[9], (N_NA_LAYERS, NA_HEADS, 2 * NA_KH - 1, 2 * NA_KW - 1), 0.1),
        'na_w_o': nrm(ks[10], (N_NA_LAYERS, na_inner, D), na_inner ** -0.5),
        'mla_w_a': nrm(ks[11], (N_MLA_LAYERS, D, MLA_Q_LORA + MLA_KV_LORA + MLA_ROPE_DIM), D ** -0.5),
        'mla_q_norm_g': 1.0 + nrm(ks[12], (N_MLA_LAYERS, MLA_Q_LORA), 0.02),
        'mla_w_qb': nrm(ks[13], (N_MLA_LAYERS, MLA_Q_LORA, MLA_HEADS * (MLA_NOPE_DIM + MLA_ROPE_DIM)), MLA_Q_LORA ** -0.5),
        'mla_kv_norm_g': 1.0 + nrm(ks[14], (N_MLA_LAYERS, MLA_KV_LORA), 0.02),
        'mla_w_kvb': nrm(ks[15], (N_MLA_LAYERS, MLA_KV_LORA, MLA_HEADS * (MLA_NOPE_DIM + MLA_V_DIM)), MLA_KV_LORA ** -0.5),
        'mla_w_o': nrm(ks[16], (N_MLA_LAYERS, MLA_HEADS * MLA_V_DIM, D), (MLA_HEADS * MLA_V_DIM) ** -0.5),
        'moe_w_router': nrm(ks[17], (DEPTH, D, N_EXPERTS), D ** -0.5),
        'moe_b_router': nrm(ks[18], (DEPTH, N_EXPERTS), 0.01),
        'moe_w1': nrm(ks[19], (DEPTH, N_EXPERTS, D, 2 * D_FF), D ** -0.5),
        'moe_b1': nrm(ks[20], (DEPTH, N_EXPERTS, 2 * D_FF), 0.01),
        'moe_w2': nrm(ks[21], (DEPTH, N_EXPERTS, D_FF, D), D_FF ** -0.5),
        'moe_b2': nrm(ks[22], (DEPTH, N_EXPERTS, D), 0.01),
        'final_norm_g': 1.0 + nrm(ks[23], (D,), 0.02),
    }


def reference(x, c, ctx, c_ctx, ada_w, ada_b, norm_mix_g, norm_ffn_g,
              na_w_qkv, na_rpb, na_w_o,
              mla_w_a, mla_q_norm_g, mla_w_qb, mla_kv_norm_g, mla_w_kvb, mla_w_o,
              moe_w_router, moe_b_router, moe_w1, moe_b1, moe_w2, moe_b2, final_norm_g):
    B, S, D = x.shape
    L = ctx.shape[1]
    silu_c = jax.nn.silu(c)
    silu_cc = jax.nn.silu(c_ctx)
    h_lat, h_ctx = x, ctx
    for i in range(DEPTH):
        need_ctx = i < DEPTH - 1
        j = i // N_MIXERS
        mod_lat = (silu_c @ ada_w[i] + ada_b[i])[:, None, :]
        mod_ctx = (silu_cc @ ada_w[i] + ada_b[i])[None, None, :]
        sh1, sc1, g1, sh2, sc2, g2 = jnp.split(mod_lat, 6, axis=-1)
        csh1, csc1, cg1, csh2, csc2, cg2 = jnp.split(mod_ctx, 6, axis=-1)

        a_lat = modulate(rmsnorm(h_lat, norm_mix_g[i]), sh1, sc1)
        a_ctx = modulate(rmsnorm(h_ctx, norm_mix_g[i]), csh1, csc1)
        if i % N_MIXERS == 0:
            o_lat, o_ctx = na_mixer(a_lat, a_ctx, na_w_qkv[j], na_rpb[j], na_w_o[j], need_ctx)
        else:
            o_lat, o_ctx = mla_mixer(a_lat, a_ctx, mla_w_a[j], mla_q_norm_g[j], mla_w_qb[j],
                                     mla_kv_norm_g[j], mla_w_kvb[j], mla_w_o[j], need_ctx)
        h_lat = h_lat + g1 * o_lat

        f_lat = modulate(rmsnorm(h_lat, norm_ffn_g[i]), sh2, sc2).reshape(B * S, D)
        if need_ctx:
            h_ctx = h_ctx + cg1 * o_ctx
            f_ctx = modulate(rmsnorm(h_ctx, norm_ffn_g[i]), csh2, csc2).reshape(B * L, D)
            y = moe(jnp.concatenate([f_lat, f_ctx], axis=0), moe_w_router[i], moe_b_router[i],
                    moe_w1[i], moe_b1[i], moe_w2[i], moe_b2[i])
            h_lat = h_lat + g2 * y[:B * S].reshape(B, S, D)
            h_ctx = h_ctx + cg2 * y[B * S:].reshape(B, L, D)
        else:
            y = moe(f_lat, moe_w_router[i], moe_b_router[i], moe_w1[i], moe_b1[i], moe_w2[i], moe_b2[i])
            h_lat = h_lat + g2 * y.reshape(B, S, D)
    return rmsnorm(h_lat, final_norm_g)
```

```python
import functools

import jax
import jax.numpy as jnp
import numpy as np
from jax import lax
from jax.experimental import pallas as pl
from jax.experimental.pallas import tpu as pltpu

F32 = jnp.float32
BF16 = jnp.bfloat16

GRID_W = 64
NORM_EPS = 1e-6
NEG = -1e30

NA_HEADS = 16
NA_HEAD_DIM = 64
NA_KH = 8
NA_KW = 16
NA_SCALE = NA_HEAD_DIM ** -0.5
NA_ROWS_PER_STEP = 4
NA_KEY_ROWS = NA_ROWS_PER_STEP + NA_KH - 1

MLA_HEADS = 16
MLA_NOPE = 64
MLA_ROPE = 32
MLA_V = 64
MLA_Q_LORA = 256
MLA_KV_LORA = 128
MLA_SCALE = (MLA_NOPE + MLA_ROPE) ** -0.5
ROPE_THETA = 10000.0

N_EXPERTS = 32
TOP_K = 4
SWIGLU_ALPHA = 1.702
SWIGLU_LIMIT = 7.0

LANES = 128
ROW_TILE = 512
MOE_ROWS = 512
TOK_TILE = 256
FLASH_TQ = 512
FLASH_TK = 512
MOD_ROWS = 16
VMEM_LIMIT = 56 * 1024 * 1024


def _params(sem, vmem=VMEM_LIMIT):
    return pltpu.CompilerParams(dimension_semantics=sem, vmem_limit_bytes=vmem)


def _dot(a, b):
    return jnp.dot(a, b, preferred_element_type=F32)


def _dot_nt(a, b):
    return lax.dot_general(a, b, (((1,), (1,)), ((), ())), preferred_element_type=F32)


def _rms(x, g):
    return x * lax.rsqrt(jnp.mean(x * x, axis=-1, keepdims=True) + NORM_EPS) * g


def _ada_body(c_ref, w_ref, b_ref, o_ref):
    c = c_ref[...]
    s = c / (1.0 + jnp.exp(-c))
    o_ref[0] = _dot(s.astype(BF16), w_ref[0].astype(BF16)) + b_ref[0]


def _ada_table(cvec, ada_w, ada_b):
    depth, d, n = ada_w.shape
    tn = 1536
    return pl.pallas_call(
        _ada_body,
        grid=(depth, n // tn),
        in_specs=[
            pl.BlockSpec((MOD_ROWS, d), lambda l, j: (0, 0)),
            pl.BlockSpec((1, d, tn), lambda l, j: (l, 0, j)),
            pl.BlockSpec((1, 1, tn), lambda l, j: (l, 0, j)),
        ],
        out_specs=pl.BlockSpec((1, MOD_ROWS, tn), lambda l, j: (l, 0, j)),
        out_shape=jax.ShapeDtypeStruct((depth, MOD_ROWS, n), F32),
        compiler_params=_params(("arbitrary", "arbitrary")),
        name="ada_table",
    )(cvec, ada_w, ada_b.reshape(depth, 1, n))


class _Rows:
    def __init__(self, n_a, n_b, tiles_per_batch, ctx_row):
        self.n_a, self.n_b, self.tpb, self.ctx_row = n_a, n_b, tiles_per_batch, ctx_row

    @property
    def n(self):
        return self.n_a + self.n_b

    def a_map(self, off):
        return lambda i: (off + jnp.minimum(i, self.n_a - 1), 0)

    def b_map(self, off):
        return lambda i: (off + jnp.maximum(i - self.n_a, 0), 0)

    def mod_map(self, chunk):
        def f(i):
            row = jnp.where(i < self.n_a, i // self.tpb, self.ctx_row)
            return (row * 6 + chunk, 0, 0)
        return f

    def pick(self, a_ref, b_ref):
        if self.n_b == 0:
            return a_ref[...]
        return jnp.where(pl.program_id(0) < self.n_a, a_ref[...], b_ref[...])


def _mod_spec(rows, chunk, d):
    return pl.BlockSpec((1, 1, d), rows.mod_map(chunk))


def _const_spec(shape):
    nd = len(shape)
    return pl.BlockSpec(shape, lambda *_: (0,) * nd)


def _nmm_body(rows, a_ref, b_ref, g_ref, sh_ref, sc_ref, w_ref, o_ref):
    x = rows.pick(a_ref, b_ref)
    a = _rms(x, g_ref[...]) * (1.0 + sc_ref[0]) + sh_ref[0]
    o_ref[...] = _dot(a.astype(BF16), w_ref[...]).astype(o_ref.dtype)


def _norm_mod_matmul(rows, src_a, off_a, src_b, off_b, g, mod3, w):
    d = src_a.shape[1]
    n = w.shape[1]
    tm = ROW_TILE
    return pl.pallas_call(
        functools.partial(_nmm_body, rows),
        grid=(rows.n,),
        in_specs=[
            pl.BlockSpec((tm, d), rows.a_map(off_a)),
            pl.BlockSpec((tm, d), rows.b_map(off_b)),
            _const_spec((1, d)),
            _mod_spec(rows, 0, d),
            _mod_spec(rows, 1, d),
            _const_spec((d, n)),
        ],
        out_specs=pl.BlockSpec((tm, n), lambda i: (i, 0)),
        out_shape=jax.ShapeDtypeStruct((rows.n * tm, n), BF16),
        compiler_params=_params(("arbitrary",)),
        name="norm_mod_matmul",
    )(src_a, src_b, g.reshape(1, d), mod3, mod3, w)


def _na_bias_table(rpb, n_rows):
    w = GRID_W
    rq, rk = NA_ROWS_PER_STEP, NA_KEY_ROWS
    dr = np.arange(rq)[:, None]
    ki = np.arange(rk)[None, :]
    ridx, rok = [], []
    for r0, kr0 in ((0, 0), (rq, 0), (n_rows - rq, n_rows - rk)):
        r = r0 + dr
        kr = kr0 + ki
        ws = np.clip(r - NA_KH // 2, 0, n_rows - NA_KH)
        rok.append((kr >= ws) & (kr < ws + NA_KH))
        ridx.append(np.clip(kr - r + NA_KH - 1, 0, 2 * NA_KH - 2))
    ridx = np.stack(ridx)
    rok = np.stack(rok)
    c = np.arange(w)[:, None]
    kc = np.arange(w)[None, :]
    wc = np.clip(c - NA_KW // 2, 0, w - NA_KW)
    cok = (kc >= wc) & (kc < wc + NA_KW)
    cidx = np.clip(kc - c + NA_KW - 1, 0, 2 * NA_KW - 2)
    h = rpb.shape[0]
    t = jnp.take(rpb, jnp.asarray(cidx.reshape(-1)), axis=2)
    t = jnp.take(t, jnp.asarray(ridx.reshape(-1)), axis=1)
    t = t.reshape(h, 3, rq, rk, w, w).transpose(1, 0, 2, 4, 3, 5)
    ok = rok[:, None, :, None, :, None] & cok[None, None, None, :, None, :]
    t = jnp.where(jnp.asarray(ok), t.astype(F32), NEG)
    return t.reshape(3, h, rq * w, rk * w)


def _softmax_pv(s_list, v_list):
    m = s_list[0].max(axis=-1, keepdims=True)
    for s in s_list[1:]:
        m = jnp.maximum(m, s.max(axis=-1, keepdims=True))
    num, den = None, None
    for s, v in zip(s_list, v_list):
        p = jnp.exp(s - m)
        l = p.sum(axis=-1, keepdims=True)
        o = _dot(p.astype(BF16), v)
        num = o if num is None else num + o
        den = l if den is None else den + l
    return num / den


def _na_body(n_rg, n_rows, q_ref, k_ref, v_ref, kc_ref, vc_ref, bias_ref, o_ref):
    rg = pl.program_id(2)
    kr0 = jnp.clip(rg * NA_ROWS_PER_STEP - NA_KH // 2, 0, n_rows - NA_KEY_ROWS)
    typ = jnp.where(rg == 0, 0, jnp.where(rg == n_rg - 1, 2, 1))
    start = pl.multiple_of(kr0 * GRID_W, GRID_W)
    nk = NA_KEY_ROWS * GRID_W
    kw = k_ref[pl.ds(start, nk), :]
    vw = v_ref[pl.ds(start, nk), :]
    kc = kc_ref[...]
    vc = vc_ref[...]
    q = q_ref[...]
    lane = lax.broadcasted_iota(jnp.int32, (1, LANES), 1)
    outs = []
    for j in range(2):
        qj = jnp.where(lane // NA_HEAD_DIM == j, q * NA_SCALE, 0).astype(BF16)
        s_loc = _dot_nt(qj, kw) + bias_ref[typ, j]
        s_ctx = _dot_nt(qj, kc)
        outs.append(_softmax_pv([s_loc, s_ctx], [vw, vc]))
    o_ref[...] = jnp.where(lane < NA_HEAD_DIM, outs[0], outs[1]).astype(o_ref.dtype)


def _na_attention(qkv, bias, batch, seq, ctx_len):
    n_rows = seq // GRID_W
    n_rg = n_rows // NA_ROWS_PER_STEP
    tq = NA_ROWS_PER_STEP * GRID_W
    n_hp = NA_HEADS // 2
    ctx_blk0 = batch * seq // ctx_len
    rq_w, rk_w = bias.shape[2], bias.shape[3]
    return pl.pallas_call(
        functools.partial(_na_body, n_rg, n_rows),
        grid=(n_hp, batch, n_rg),
        in_specs=[
            pl.BlockSpec((tq, LANES), lambda hp, b, rg: (b * n_rg + rg, hp)),
            pl.BlockSpec((seq, LANES), lambda hp, b, rg: (b, n_hp + hp)),
            pl.BlockSpec((seq, LANES), lambda hp, b, rg: (b, 2 * n_hp + hp)),
            pl.BlockSpec((ctx_len, LANES), lambda hp, b, rg: (ctx_blk0 + b, n_hp + hp)),
            pl.BlockSpec((ctx_len, LANES), lambda hp, b, rg: (ctx_blk0 + b, 2 * n_hp + hp)),
            pl.BlockSpec((3, 2, rq_w, rk_w), lambda hp, b, rg: (0, hp, 0, 0)),
        ],
        out_specs=pl.BlockSpec((tq, LANES), lambda hp, b, rg: (b * n_rg + rg, hp)),
        out_shape=jax.ShapeDtypeStruct((batch * seq, NA_HEADS * NA_HEAD_DIM), BF16),
        compiler_params=_params(("arbitrary", "arbitrary", "arbitrary")),
        name="na_attention",
    )(qkv, qkv, qkv, qkv, qkv, bias)


def _na_ctx_body(q_ref, k_ref, v_ref, o_ref):
    q = q_ref[...]
    k = k_ref[...]
    v = v_ref[...]
    lane = lax.broadcasted_iota(jnp.int32, (1, LANES), 1)
    outs = []
    for j in range(2):
        qj = jnp.where(lane // NA_HEAD_DIM == j, q * NA_SCALE, 0).astype(BF16)
        outs.append(_softmax_pv([_dot_nt(qj, k)], [v]))
    o_ref[...] = jnp.where(lane < NA_HEAD_DIM, outs[0], outs[1]).astype(o_ref.dtype)


def _na_ctx_attention(qkv, batch, seq, ctx_len):
    n_hp = NA_HEADS // 2
    blk0 = batch * seq // ctx_len
    return pl.pallas_call(
        _na_ctx_body,
        grid=(n_hp, batch),
        in_specs=[
            pl.BlockSpec((ctx_len, LANES), lambda hp, b: (blk0 + b, hp)),
            pl.BlockSpec((ctx_len, LANES), lambda hp, b: (blk0 + b, n_hp + hp)),
            pl.BlockSpec((ctx_len, LANES), lambda hp, b: (blk0 + b, 2 * n_hp + hp)),
        ],
        out_specs=pl.BlockSpec((ctx_len, LANES), lambda hp, b: (b, hp)),
        out_shape=jax.ShapeDtypeStruct((batch * ctx_len, NA_HEADS * NA_HEAD_DIM), BF16),
        compiler_params=_params(("arbitrary", "arbitrary")),
        name="na_ctx_attention",
    )(qkv, qkv, qkv)


def _prn_body(rows, tm, oa_ref, ob_ref, ha_ref, hb_ref, wo_ref, g1_ref, gf_ref, sh_ref, sc_ref,
              wr_ref, br_ref, h_out, f_out, r_out, cnt_out, carry):
    i = pl.program_id(0)

    @pl.when(i == 0)
    def _():
        carry[...] = jnp.zeros_like(carry)

    o = rows.pick(oa_ref, ob_ref)
    h = rows.pick(ha_ref, hb_ref)
    hn = h + g1_ref[0] * _dot(o, wo_ref[...])
    h_out[...] = hn
    f = _rms(hn, gf_ref[...]) * (1.0 + sc_ref[0]) + sh_ref[0]
    f_out[...] = f
    logits = _dot(f.astype(BF16), wr_ref[...]) + br_ref[...]

    lane = lax.broadcasted_iota(jnp.int32, (tm, LANES), 1).astype(F32)
    vals, hot = [], []
    l = logits
    for _ in range(TOP_K):
        m = l.max(axis=-1, keepdims=True)
        idx = jnp.where(l == m, lane, float(LANES)).min(axis=-1, keepdims=True)
        sel = lane == idx
        vals.append(m)
        hot.append(sel)
        l = jnp.where(sel, -3e38, l)
    ex = [jnp.exp(v - vals[0]) for v in vals]
    den = ex[0] + ex[1] + ex[2] + ex[3]

    onehot = [jnp.where(s, 1.0, 0.0) for s in hot]
    tot = onehot[0] + onehot[1] + onehot[2] + onehot[3]
    rr = lax.broadcasted_iota(jnp.int32, (tm, tm), 0)
    cc = lax.broadcasted_iota(jnp.int32, (tm, tm), 1)
    tril = jnp.where(rr > cc, 1.0, 0.0).astype(BF16)
    before = _dot(tril, tot.astype(BF16)) + carry[0:1, :]
    carry[...] = carry[...] + tot.sum(axis=0, keepdims=True)
    cnt_out[...] = carry[...]

    r = jnp.zeros((tm, LANES), F32)
    for k in range(TOP_K):
        idx_k = (onehot[k] * lane).sum(axis=-1, keepdims=True)
        rank_k = (onehot[k] * before).sum(axis=-1, keepdims=True)
        r = jnp.where(lane == k, ex[k] / den, r)
        r = jnp.where(lane == TOP_K + k, idx_k, r)
        r = jnp.where(lane == 2 * TOP_K + k, rank_k, r)
    r_out[...] = r


def _proj_res_norm_router(rows, o_a, o_b, h_a, off_ha, h_b, off_hb, w_o, g_ffn, mod3, w_r, b_r):
    d = h_a.shape[1]
    tm = ROW_TILE
    n_tok = rows.n * tm
    e = w_r.shape[1]
    wr = jnp.zeros((d, LANES), BF16).at[:, :e].set(w_r.astype(BF16))
    br = jnp.full((1, LANES), NEG, F32).at[0, :e].set(b_r)
    return pl.pallas_call(
        functools.partial(_prn_body, rows, tm),
        grid=(rows.n,),
        in_specs=[
            pl.BlockSpec((tm, d), rows.a_map(0)),
            pl.BlockSpec((tm, d), rows.b_map(0)),
            pl.BlockSpec((tm, d), rows.a_map(off_ha)),
            pl.BlockSpec((tm, d), rows.b_map(off_hb)),
            _const_spec((d, d)),
            _mod_spec(rows, 2, d),
            _const_spec((1, d)),
            _mod_spec(rows, 3, d),
            _mod_spec(rows, 4, d),
            _const_spec((d, LANES)),
            _const_spec((1, LANES)),
        ],
        out_specs=[
            pl.BlockSpec((tm, d), lambda i: (i, 0)),
            pl.BlockSpec((tm, d), lambda i: (i, 0)),
            pl.BlockSpec((tm, LANES), lambda i: (i, 0)),
            _const_spec((8, LANES)),
        ],
        out_shape=[
            jax.ShapeDtypeStruct((n_tok, d), F32),
            jax.ShapeDtypeStruct((n_tok, d), F32),
            jax.ShapeDtypeStruct((n_tok, LANES), F32),
            jax.ShapeDtypeStruct((8, LANES), F32),
        ],
        scratch_shapes=[pltpu.VMEM((8, LANES), F32)],
        compiler_params=_params(("arbitrary",)),
        name="proj_res_norm_router",
    )(o_a, o_b, h_a, h_b, w_o, mod3, g_ffn.reshape(1, d), mod3, mod3, wr, br)


def _route_plan(route, cnt, n_tok):
    idx = route[:, TOP_K:2 * TOP_K].astype(jnp.int32)
    rank = route[:, 2 * TOP_K:3 * TOP_K].astype(jnp.int32)
    counts = cnt[0, :N_EXPERTS].astype(jnp.int32)
    ends = jnp.cumsum(counts)
    starts = ends - counts
    ex = jnp.arange(N_EXPERTS, dtype=jnp.int32)
    dest = jnp.sum(jnp.where(idx[..., None] == ex, starts, 0), axis=-1) + rank

    bm = MOE_ROWS
    n_blocks = n_tok * TOP_K // bm
    n_items = n_blocks + N_EXPERTS - 1
    first_blk = starts // bm
    last_blk = jnp.maximum(ends - 1, 0) // bm
    nblk = jnp.where(counts > 0, last_blk - first_blk + 1, 0)
    item_end = jnp.cumsum(nblk)
    item_start = item_end - nblk
    it = jnp.arange(n_items, dtype=jnp.int32)
    e_raw = jnp.sum(it[:, None] >= item_end[None, :], axis=1).astype(jnp.int32)
    valid = e_raw < N_EXPERTS
    last_e = jnp.max(jnp.where(counts > 0, ex, 0))
    e_i = jnp.where(valid, jnp.minimum(e_raw, N_EXPERTS - 1), last_e)
    blk = jnp.where(valid, first_blk[e_i] + it - item_start[e_i], n_blocks - 1)
    lo = jnp.where(valid, jnp.maximum(starts[e_i], blk * bm) - blk * bm, 0)
    hi = jnp.where(valid, jnp.minimum(ends[e_i], (blk + 1) * bm) - blk * bm, 0)
    prev_blk = jnp.concatenate([jnp.full((1,), -1, jnp.int32), blk[:-1]])
    first = (blk != prev_blk).astype(jnp.int32)
    return dest, (blk.astype(jnp.int32), e_i.astype(jnp.int32), lo.astype(jnp.int32),
                  hi.astype(jnp.int32), first)


def _row_copy(src, s, dst, d, sem):
    return pltpu.make_async_copy(src.at[pl.ds(s, 1)], dst.at[pl.ds(d, 1)], sem)


def _dispatch_body(tm, dest_ref, f_ref, x_hbm, sem):
    n = tm * TOP_K

    def issue(a, c):
        _row_copy(f_ref, a // TOP_K, x_hbm, dest_ref[0, 0, a], sem).start()
        return c

    lax.fori_loop(0, n, issue, 0)

    def drain(a, c):
        _row_copy(f_ref, 0, x_hbm, 0, sem).wait()
        return c

    lax.fori_loop(0, n, drain, 0)


def _dispatch(f, dest):
    n_tok, d = f.shape
    tm = TOK_TILE
    n_tiles = n_tok // tm
    dest3 = dest.reshape(n_tiles, 1, tm * TOP_K)
    return pl.pallas_call(
        functools.partial(_dispatch_body, tm),
        grid=(n_tiles,),
        in_specs=[
            pl.BlockSpec((1, 1, tm * TOP_K), lambda i: (i, 0, 0), memory_space=pltpu.SMEM),
            pl.BlockSpec((tm, d), lambda i: (i, 0)),
        ],
        out_specs=pl.BlockSpec(memory_space=pl.ANY),
        out_shape=jax.ShapeDtypeStruct((n_tok * TOP_K, d), f.dtype),
        scratch_shapes=[pltpu.SemaphoreType.DMA(())],
        compiler_params=_params(("arbitrary",)),
        name="moe_dispatch",
    )(dest3, f)


def _gmm_body(blk_ref, e_ref, lo_ref, hi_ref, first_ref, x_ref, w1_ref, b1_ref, w2_ref, b2_ref, o_ref):
    i = pl.program_id(0)
    lo = lo_ref[i]
    hi = hi_ref[i]

    @pl.when(hi > lo)
    def _():
        dff = w2_ref.shape[1]
        h = _dot(x_ref[...].astype(BF16), w1_ref[0]) + b1_ref[0]
        hg = jnp.minimum(h[:, :dff], SWIGLU_LIMIT)
        hl = jnp.clip(h[:, dff:], -SWIGLU_LIMIT, SWIGLU_LIMIT)
        act = hg / (1.0 + jnp.exp(-SWIGLU_ALPHA * hg)) * (hl + 1.0)
        y = _dot(act.astype(BF16), w2_ref[0]) + b2_ref[0]
        row = lax.broadcasted_iota(jnp.int32, (y.shape[0], 1), 0)
        mine = (row >= lo) & (row < hi)

        @pl.when(first_ref[i] == 1)
        def _():
            o_ref[...] = jnp.where(mine, y, 0.0)

        @pl.when(first_ref[i] != 1)
        def _():
            o_ref[...] = jnp.where(mine, y, o_ref[...])


def _grouped_mlp(x_sorted, plan, w1, b1, w2, b2):
    n_rows, d = x_sorted.shape
    bm = MOE_ROWS
    n_items = plan[0].shape[0]
    dff = w2.shape[1]
    grid_spec = pltpu.PrefetchScalarGridSpec(
        num_scalar_prefetch=5,
        grid=(n_items,),
        in_specs=[
            pl.BlockSpec((bm, d), lambda i, blk, e, lo, hi, fi: (blk[i], 0)),
            pl.BlockSpec((1, d, 2 * dff), lambda i, blk, e, lo, hi, fi: (e[i], 0, 0)),
            pl.BlockSpec((1, 1, 2 * dff), lambda i, blk, e, lo, hi, fi: (e[i], 0, 0)),
            pl.BlockSpec((1, dff, d), lambda i, blk, e, lo, hi, fi: (e[i], 0, 0)),
            pl.BlockSpec((1, 1, d), lambda i, blk, e, lo, hi, fi: (e[i], 0, 0)),
        ],
        out_specs=pl.BlockSpec((bm, d), lambda i, blk, e, lo, hi, fi: (blk[i], 0)),
    )
    return pl.pallas_call(
        _gmm_body,
        grid_spec=grid_spec,
        out_shape=jax.ShapeDtypeStruct((n_rows, d), F32),
        compiler_params=_params(("arbitrary",)),
        name="moe_grouped_mlp",
    )(*plan, x_sorted, w1, b1, w2, b2)


def _combine_body(tm, final_norm, dest_ref, y_hbm, r_ref, h_ref, g2_ref, gn_ref, o_ref, buf, sem):
    n = tm * TOP_K

    def issue(a, c):
        _row_copy(y_hbm, dest_ref[0, 0, a], buf, (a % TOP_K) * tm + a // TOP_K, sem).start()
        return c

    lax.fori_loop(0, n, issue, 0)

    def drain(a, c):
        _row_copy(y_hbm, 0, buf, 0, sem).wait()
        return c

    lax.fori_loop(0, n, drain, 0)

    r = r_ref[...]
    y = r[:, 0:1] * buf[0:tm, :]
    for k in range(1, TOP_K):
        y = y + r[:, k:k + 1] * buf[k * tm:(k + 1) * tm, :]
    hn = h_ref[...] + g2_ref[0] * y
    if final_norm:
        hn = _rms(hn, gn_ref[...])
    o_ref[...] = hn


def _combine(rows, y_sorted, dest, route, h, mod3, g_final, final_norm):
    n_tok, d = h.shape
    tm = TOK_TILE
    n_tiles = n_tok // tm
    dest3 = dest.reshape(n_tiles, 1, tm * TOP_K)
    return pl.pallas_call(
        functools.partial(_combine_body, tm, final_norm),
        grid=(n_tiles,),
        in_specs=[
            pl.BlockSpec((1, 1, tm * TOP_K), lambda i: (i, 0, 0), memory_space=pltpu.SMEM),
            pl.BlockSpec(memory_space=pl.ANY),
            pl.BlockSpec((tm, LANES), lambda i: (i, 0)),
            pl.BlockSpec((tm, d), lambda i: (i, 0)),
            _mod_spec(rows, 5, d),
            _const_spec((1, d)),
        ],
        out_specs=pl.BlockSpec((tm, d), lambda i: (i, 0)),
        out_shape=jax.ShapeDtypeStruct((n_tok, d), F32),
        scratch_shapes=[pltpu.VMEM((TOP_K * tm, d), F32), pltpu.SemaphoreType.DMA(())],
        compiler_params=_params(("arbitrary",)),
        name="moe_combine",
    )(dest3, y_sorted, route, h, mod3, g_final.reshape(1, d))


def _moe_weights(w1, b1, w2, b2):
    w1p = jnp.concatenate([w1[..., 0::2], w1[..., 1::2]], axis=-1).astype(BF16)
    b1p = jnp.concatenate([b1[..., 0::2], b1[..., 1::2]], axis=-1)[:, None, :]
    return w1p, b1p, w2.astype(BF16), b2[:, None, :]


def _moe(rows_tok, h, f, route, cnt, mod3, w1, b1, w2, b2, g_final, final_norm):
    n_tok = h.shape[0]
    dest, plan = _route_plan(route, cnt, n_tok)
    x_sorted = _dispatch(f, dest)
    y_sorted = _grouped_mlp(x_sorted, plan, *_moe_weights(w1, b1, w2, b2))
    return _combine(rows_tok, y_sorted, dest, route, h, mod3, g_final, final_norm)


def _partner(w):
    wp = w.reshape(w.shape[0], -1, 2)
    return jnp.stack([-wp[..., 1], wp[..., 0]], axis=-1).reshape(w.shape)


def _mla_weights(w_a, w_qb, w_kvb):
    d = w_a.shape[0]
    h = MLA_HEADS
    r0 = MLA_NOPE
    r1 = MLA_NOPE + MLA_ROPE
    kr = w_a[:, MLA_Q_LORA + MLA_KV_LORA:]
    wa = jnp.zeros((d, MLA_Q_LORA + MLA_KV_LORA + 2 * LANES), F32)
    wa = wa.at[:, :MLA_Q_LORA + MLA_KV_LORA].set(w_a[:, :MLA_Q_LORA + MLA_KV_LORA])
    base = MLA_Q_LORA + MLA_KV_LORA
    wa = wa.at[:, base + r0:base + r1].set(kr)
    wa = wa.at[:, base + LANES + r0:base + LANES + r1].set(_partner(kr))
    wq = w_qb.reshape(MLA_Q_LORA, h, r1)
    wq1 = jnp.zeros((MLA_Q_LORA, h, LANES), F32).at[:, :, :r1].set(wq)
    wq2 = jnp.zeros((MLA_Q_LORA, h, LANES), F32).at[:, :, r0:r1].set(
        _partner(wq[:, :, r0:].reshape(MLA_Q_LORA, h * MLA_ROPE)).reshape(MLA_Q_LORA, h, MLA_ROPE))
    wkv = w_kvb.reshape(MLA_KV_LORA, h, MLA_NOPE + MLA_V)
    wkn = jnp.zeros((MLA_KV_LORA, h, LANES), F32).at[:, :, :MLA_NOPE].set(wkv[:, :, :MLA_NOPE])
    wv = jnp.zeros((MLA_KV_LORA, h, LANES), F32).at[:, :, :MLA_V].set(wkv[:, :, MLA_NOPE:])
    flat = lambda w: w.reshape(w.shape[0], h * LANES).astype(BF16)
    return wa.astype(BF16), flat(wq1), flat(wq2), flat(wkn), flat(wv)


def _rope_tables(seq, tm):
    t = np.arange(seq)
    pos = np.stack([t // GRID_W, t % GRID_W], axis=1).astype(np.float32)
    axis_dim = MLA_ROPE // 2
    inv = jnp.asarray(ROPE_THETA, F32) ** (-jnp.arange(0, axis_dim, 2, dtype=F32) / axis_dim)
    ang = jnp.concatenate([jnp.asarray(pos[:, 0:1]) * inv, jnp.asarray(pos[:, 1:2]) * inv], axis=-1)
    cos = jnp.repeat(jnp.cos(ang), 2, axis=-1)
    sin = jnp.repeat(jnp.sin(ang), 2, axis=-1)
    cos = jnp.concatenate([cos, jnp.ones((tm, MLA_ROPE), F32)], axis=0)
    sin = jnp.concatenate([sin, jnp.zeros((tm, MLA_ROPE), F32)], axis=0)
    n = seq + tm
    pad = jnp.zeros((n, LANES - MLA_NOPE - MLA_ROPE), F32)
    ck = jnp.concatenate([jnp.zeros((n, MLA_NOPE), F32), cos, pad], axis=-1)
    sk = jnp.concatenate([jnp.zeros((n, MLA_NOPE), F32), sin, pad], axis=-1)
    cq = jnp.concatenate([jnp.ones((n, MLA_NOPE), F32), cos, pad], axis=-1) * MLA_SCALE
    sq = sk * MLA_SCALE
    return cq, sq, ck, sk


def _mla_proj_body(rows, a_ref, b_ref, g_ref, sh_ref, sc_ref, wa_ref, qg_ref, kvg_ref,
                   wq1_ref, wq2_ref, wkn_ref, wv_ref, cq_ref, sq_ref, ck_ref, sk_ref,
                   q_out, k_out, v_out):
    x = rows.pick(a_ref, b_ref)
    a = _rms(x, g_ref[...]) * (1.0 + sc_ref[0]) + sh_ref[0]
    y = _dot(a.astype(BF16), wa_ref[...])
    c0 = MLA_Q_LORA
    c1 = c0 + MLA_KV_LORA
    qn = _rms(y[:, :c0], qg_ref[...]).astype(BF16)
    kvn = _rms(y[:, c0:c1], kvg_ref[...]).astype(BF16)
    k_rope = y[:, c1:c1 + LANES] * ck_ref[...] + y[:, c1 + LANES:c1 + 2 * LANES] * sk_ref[...]
    y1 = _dot(qn, wq1_ref[...])
    y2 = _dot(qn, wq2_ref[...])
    kn = _dot(kvn, wkn_ref[...])
    vv = _dot(kvn, wv_ref[...])
    cq = cq_ref[...]
    sq = sq_ref[...]
    lane = lax.broadcasted_iota(jnp.int32, (1, LANES), 1)
    one_col = jnp.where(lane == MLA_V, 1.0, 0.0)
    for h in range(MLA_HEADS):
        hs = slice(h * LANES, (h + 1) * LANES)
        q_out[:, hs] = (y1[:, hs] * cq + y2[:, hs] * sq).astype(q_out.dtype)
        k_out[:, hs] = (kn[:, hs] + k_rope).astype(k_out.dtype)
        v_out[:, hs] = (vv[:, hs] + one_col).astype(v_out.dtype)


def _mla_project(rows, src, off_a, off_b, g, mod3, weights, q_g, kv_g, tables, seq):
    d = src.shape[1]
    tm = ROW_TILE
    wa, wq1, wq2, wkn, wv = weights
    hw = MLA_HEADS * LANES
    tpb = seq // tm

    def tab_map(i):
        return (jnp.where(i < rows.n_a, i % tpb, tpb), 0)

    tab_spec = pl.BlockSpec((tm, LANES), tab_map)
    out_spec = pl.BlockSpec((tm, hw), lambda i: (i, 0))
    out_sds = jax.ShapeDtypeStruct((rows.n * tm, hw), BF16)
    return pl.pallas_call(
        functools.partial(_mla_proj_body, rows),
        grid=(rows.n,),
        in_specs=[
            pl.BlockSpec((tm, d), rows.a_map(off_a)),
            pl.BlockSpec((tm, d), rows.b_map(off_b)),
            _const_spec((1, d)),
            _mod_spec(rows, 0, d),
            _mod_spec(rows, 1, d),
            _const_spec(wa.shape),
            _const_spec((1, MLA_Q_LORA)),
            _const_spec((1, MLA_KV_LORA)),
            _const_spec(wq1.shape),
            _const_spec(wq2.shape),
            _const_spec(wkn.shape),
            _const_spec(wv.shape),
            tab_spec, tab_spec, tab_spec, tab_spec,
        ],
        out_specs=[out_spec, out_spec, out_spec],
        out_shape=[out_sds, out_sds, out_sds],
        compiler_params=_params(("arbitrary",)),
        name="mla_project",
    )(src, src, g.reshape(1, d), mod3, mod3, wa, q_g.reshape(1, -1), kv_g.reshape(1, -1),
      wq1, wq2, wkn, wv, *tables)


def _flash_update(q, k, v, m, acc):
    s = _dot_nt(q, k)
    m_new = jnp.maximum(m, s.max(axis=-1, keepdims=True))
    p = jnp.exp(s - m_new)
    return m_new, jnp.exp(m - m_new) * acc + _dot(p.astype(BF16), v)


def _flash_body(n_chunks, tk, q_ref, k_ref, v_ref, kc_ref, vc_ref, o_ref):
    tq = q_ref.shape[0]
    outs = []
    for j in range(2):
        hs = slice(j * LANES, (j + 1) * LANES)
        q = q_ref[:, hs]

        def step(c, carry):
            off = pl.multiple_of(c * tk, tk)
            return _flash_update(q, k_ref[pl.ds(off, tk), hs], v_ref[pl.ds(off, tk), hs], *carry)

        m, acc = lax.fori_loop(0, n_chunks, step,
                               (jnp.full((tq, 1), NEG, F32), jnp.zeros((tq, LANES), F32)))
        m, acc = _flash_update(q, kc_ref[:, hs], vc_ref[:, hs], m, acc)
        outs.append(acc[:, :MLA_V] / acc[:, MLA_V:MLA_V + 1])
    o_ref[...] = jnp.concatenate(outs, axis=-1).astype(o_ref.dtype)


def _flash_attention(q, k, v, batch, seq, ctx_len):
    tq, tk = FLASH_TQ, FLASH_TK
    n_hp = MLA_HEADS // 2
    n_q = seq // tq
    ctx_blk0 = batch * seq // ctx_len
    w = 2 * LANES
    return pl.pallas_call(
        functools.partial(_flash_body, seq // tk, tk),
        grid=(batch, n_hp, n_q),
        in_specs=[
            pl.BlockSpec((tq, w), lambda b, hp, i: (b * n_q + i, hp)),
            pl.BlockSpec((seq, w), lambda b, hp, i: (b, hp)),
            pl.BlockSpec((seq, w), lambda b, hp, i: (b, hp)),
            pl.BlockSpec((ctx_len, w), lambda b, hp, i: (ctx_blk0 + b, hp)),
            pl.BlockSpec((ctx_len, w), lambda b, hp, i: (ctx_blk0 + b, hp)),
        ],
        out_specs=pl.BlockSpec((tq, 2 * MLA_V), lambda b, hp, i: (b * n_q + i, hp)),
        out_shape=jax.ShapeDtypeStruct((batch * seq, MLA_HEADS * MLA_V), BF16),
        compiler_params=_params(("arbitrary", "arbitrary", "arbitrary")),
        name="mla_flash_attention",
    )(q, k, v, k, v)


def kernel(x, c, ctx, c_ctx, ada_w, ada_b, norm_mix_g, norm_ffn_g, na_w_qkv, na_rpb, na_w_o, mla_w_a, mla_q_norm_g, mla_w_qb, mla_kv_norm_g, mla_w_kvb, mla_w_o, moe_w_router, moe_b_router, moe_w1, moe_b1, moe_w2, moe_b2, final_norm_g):
    batch, seq, d = x.shape
    ctx_len = ctx.shape[1]
    depth = ada_w.shape[0]
    assert depth == 2 and batch < MOD_ROWS and d == NA_HEADS * NA_HEAD_DIM
    assert seq % ROW_TILE == 0 and (batch * ctx_len) % ROW_TILE == 0 and seq % ctx_len == 0
    n_lat, n_ctx = batch * seq, batch * ctx_len
    tm = ROW_TILE
    lat_tiles, ctx_tiles = n_lat // tm, n_ctx // tm

    cvec = jnp.zeros((MOD_ROWS, d), F32).at[:batch].set(c).at[batch].set(c_ctx)
    mod = _ada_table(cvec, ada_w, ada_b)
    mod3 = [mod[l].reshape(MOD_ROWS * 6, 1, d) for l in range(depth)]

    x2 = x.reshape(n_lat, d)
    ctx2 = ctx.reshape(n_ctx, d)
    rows_all = _Rows(lat_tiles, ctx_tiles, seq // tm, batch)
    rows_lat = _Rows(lat_tiles, 0, seq // tm, batch)
    tok_all = _Rows(n_lat // TOK_TILE, n_ctx // TOK_TILE, seq // TOK_TILE, batch)
    tok_lat = _Rows(n_lat // TOK_TILE, 0, seq // TOK_TILE, batch)

    qkv = _norm_mod_matmul(rows_all, x2, 0, ctx2, 0, norm_mix_g[0], mod3[0], na_w_qkv[0].astype(BF16))
    bias = _na_bias_table(na_rpb[0], seq // GRID_W)
    o_lat = _na_attention(qkv, bias, batch, seq, ctx_len)
    o_ctx = _na_ctx_attention(qkv, batch, seq, ctx_len)
    h, f, route, cnt = _proj_res_norm_router(
        rows_all, o_lat, o_ctx, x2, 0, ctx2, 0, na_w_o[0].astype(BF16), norm_ffn_g[0], mod3[0],
        moe_w_router[0], moe_b_router[0])
    h = _moe(tok_all, h, f, route, cnt, mod3[0], moe_w1[0], moe_b1[0], moe_w2[0], moe_b2[0],
             final_norm_g, False)

    q, k, v = _mla_project(rows_all, h, 0, lat_tiles, norm_mix_g[1], mod3[1],
                           _mla_weights(mla_w_a[0], mla_w_qb[0], mla_w_kvb[0]),
                           mla_q_norm_g[0], mla_kv_norm_g[0], _rope_tables(seq, tm), seq)
    o_lat = _flash_attention(q, k, v, batch, seq, ctx_len)
    h1, f, route, cnt = _proj_res_norm_router(
        rows_lat, o_lat, o_lat, h, 0, h, 0, mla_w_o[0].astype(BF16), norm_ffn_g[1], mod3[1],
        moe_w_router[1], moe_b_router[1])
    out = _moe(tok_lat, h1, f, route, cnt, mod3[1], moe_w1[1], moe_b1[1], moe_w2[1], moe_b2[1],
               final_norm_g, True)
    return out.reshape(batch, seq, d)
```
